```python
import math, functools
import jax, jax.numpy as jnp
from jax import lax
import numpy as np

D_MODEL = 2048
BATCH = 4
SEQ = 2048
DEPTH = 4
DEC_BATCH = 8
DEC_SEQ = 4
PAST_LEN = 16384
PAGE_SIZE = 128

N_A_LAYERS = DEPTH // 2
N_B_LAYERS = DEPTH - N_A_LAYERS
RET_HEADS = 8
RET_QK_DIM = D_MODEL
RET_V_DIM = 2 * D_MODEL
RET_DK = RET_QK_DIM // RET_HEADS
RET_DV = RET_V_DIM // RET_HEADS
RET_CHUNK = 128
ROPE_BASE = 10000.0
DIFF_HEADS = 8
DIFF_HD = D_MODEL // DIFF_HEADS // 2
N_QK_HEADS = 2 * DIFF_HEADS
DIFF_VD = 2 * DIFF_HD
Q_BLOCK = 128
N_BUCKETS = 32
MAX_DISTANCE = 128
D_FF = 4 * D_MODEL
EPS = 1e-6
NEG_INF = -1e30

kernel_name = "yoco_retnet_diffattn_decoder_step"


def rmsnorm(x, g):
    xf = x.astype(jnp.float32)
    y = xf * lax.rsqrt(jnp.mean(xf * xf, -1, keepdims=True) + EPS)
    return (y * g.astype(jnp.float32)).astype(x.dtype)


def rope(x, pos):
    half = x.shape[-1] // 2
    inv = ROPE_BASE ** (-jnp.arange(half, dtype=jnp.float32) / half)
    ang = pos.astype(jnp.float32)[:, None] * inv[None, :]
    cos = jnp.cos(ang)[None, :, None, :]
    sin = jnp.sin(ang)[None, :, None, :]
    xf = x.astype(jnp.float32)
    x1, x2 = xf[..., :half], xf[..., half:]
    return jnp.concatenate([x1 * cos - x2 * sin, x1 * sin + x2 * cos], -1).astype(x.dtype)


def retention_log_decay():
    return jnp.log1p(-jnp.exp2(-5.0 - jnp.arange(RET_HEADS, dtype=jnp.float32)))


def retention_chunk(S, q, k, v, log_decay):
    L = q.shape[1]
    idx = jnp.arange(L, dtype=jnp.float32)
    rel = idx[:, None] - idx[None, :]
    dmask = jnp.where(rel[None] >= 0, jnp.exp(log_decay[:, None, None] * jnp.maximum(rel, 0.0)[None]), 0.0)
    qf, kf, vf = q.astype(jnp.float32), k.astype(jnp.float32), v.astype(jnp.float32)
    scores = jnp.einsum('bihd,bjhd->bhij', qf, kf) * dmask[None]
    o = jnp.einsum('bhij,bjhe->bihe', scores, vf)
    cross_decay = jnp.exp(log_decay[None, :] * (idx[:, None] + 1.0))
    o = o + jnp.einsum('bihd,bhde->bihe', qf, S) * cross_decay[None, :, :, None]
    k_decay = jnp.exp(log_decay[None, :] * (L - 1.0 - idx)[:, None])
    S = S * jnp.exp(log_decay * L)[None, :, None, None] + jnp.einsum('bjhd,bjhe->bhde', kf * k_decay[None, :, :, None], vf)
    return S, o


def retention_scan(S0, q, k, v, log_decay):
    B, T = q.shape[:2]
    C = RET_CHUNK if T % RET_CHUNK == 0 else T
    nc = T // C

    def split(a):
        return a.reshape(B, nc, C, *a.shape[2:]).swapaxes(0, 1)

    S, o = lax.scan(lambda s, xs: retention_chunk(s, *xs, log_decay), S0.astype(jnp.float32), (split(q), split(k), split(v)))
    o = o.swapaxes(0, 1).reshape(B, T, RET_HEADS, RET_DV)
    return o, S


def retention_mixer(h, pos, S0, w_in, w_out):
    B, T, _ = h.shape
    proj = h @ w_in
    q, k, v, g = jnp.split(proj, [RET_QK_DIM, 2 * RET_QK_DIM, 2 * RET_QK_DIM + RET_V_DIM], axis=-1)
    q = rope(q.reshape(B, T, RET_HEADS, RET_DK), pos)
    k = rope(k.reshape(B, T, RET_HEADS, RET_DK), pos) * (RET_DK ** -0.5)
    v = v.reshape(B, T, RET_HEADS, RET_DV)
    o, S = retention_scan(S0, q, k, v, retention_log_decay())
    o = o * lax.rsqrt(jnp.mean(o * o, -1, keepdims=True) + EPS)
    o = jax.nn.silu(g.astype(jnp.float32)) * o.reshape(B, T, RET_V_DIM)
    return o.astype(h.dtype) @ w_out, S.astype(S0.dtype)


def rel_bias(qpos, kpos, table):
    n = jnp.maximum(qpos[:, None] - kpos[None, :], 0)
    max_exact = N_BUCKETS // 2
    nf = jnp.maximum(n, max_exact).astype(jnp.float32)
    large = max_exact + (jnp.log(nf / max_exact) / math.log(MAX_DISTANCE / max_exact) * (N_BUCKETS - max_exact)).astype(jnp.int32)
    bucket = jnp.where(n < max_exact, n, jnp.minimum(large, N_BUCKETS - 1))
    return jnp.moveaxis(table[bucket].astype(jnp.float32), -1, 0)


def diff_weights(logits, lam):
    p = jax.nn.softmax(logits, axis=-1)
    p = p.reshape(p.shape[0], DIFF_HEADS, 2, *p.shape[2:])
    return p[:, :, 0] - lam * p[:, :, 1]


def attend_prompt(q, k, v, lam, table):
    B, S = q.shape[:2]
    nb = S // Q_BLOCK
    qb = q.reshape(B, nb, Q_BLOCK, N_QK_HEADS, DIFF_HD).swapaxes(0, 1)
    kpos = jnp.arange(S)

    def block(args):
        qi, bi = args
        qpos = bi * Q_BLOCK + jnp.arange(Q_BLOCK)
        logits = jnp.einsum('bqhd,bkhd->bhqk', qi, k).astype(jnp.float32) + rel_bias(qpos, kpos, table)[None]
        logits = jnp.where((kpos[None, :] <= qpos[:, None])[None, None], logits, NEG_INF)
        a = diff_weights(logits, lam)
        return jnp.einsum('bhqk,bkhe->bqhe', a.astype(v.dtype), v)

    out = lax.map(block, (qb, jnp.arange(nb)))
    return out.swapaxes(0, 1).reshape(B, S, DIFF_HEADS, DIFF_VD)


def attend_sample(q, k, v, lam, table, k_past, v_past):
    T = q.shape[1]
    P = k_past.shape[1]
    qpos = P + jnp.arange(T)
    lp = jnp.einsum('bqhd,bkhd->bhqk', q, k_past).astype(jnp.float32) + rel_bias(qpos, jnp.arange(P), table)[None]
    ln = jnp.einsum('bqhd,bkhd->bhqk', q, k).astype(jnp.float32) + rel_bias(qpos, qpos, table)[None]
    ln = jnp.where((qpos[None, :] <= qpos[:, None])[None, None], ln, NEG_INF)
    a = diff_weights(jnp.concatenate([lp, ln], axis=-1), lam)
    return (jnp.einsum('bhqk,bkhe->bqhe', a[..., :P].astype(v_past.dtype), v_past)
            + jnp.einsum('bhqk,bkhe->bqhe', a[..., P:].astype(v.dtype), v))


def lambda_init(layer_idx):
    return 0.8 - 0.6 * math.exp(-0.3 * layer_idx)


def diff_mixer(h, k, v, attend, w_q, lam_p, sub_g, w_o, lam_init):
    B, T, _ = h.shape
    q = (h @ w_q).reshape(B, T, N_QK_HEADS, DIFF_HD) * (DIFF_HD ** -0.5)
    lp = lam_p.astype(jnp.float32)
    lam = jnp.exp(jnp.sum(lp[0] * lp[1])) - jnp.exp(jnp.sum(lp[2] * lp[3])) + lam_init
    o = attend(q, k, v, lam)
    o = rmsnorm(o, sub_g) * (1.0 - lam_init)
    return o.reshape(B, T, DIFF_HEADS * DIFF_VD) @ w_o


def shared_kv(x, kv_norm, w_kv):
    B, T, _ = x.shape
    kv = rmsnorm(x, kv_norm) @ w_kv
    k, v = jnp.split(kv, [N_QK_HEADS * DIFF_HD], axis=-1)
    return k.reshape(B, T, N_QK_HEADS, DIFF_HD), v.reshape(B, T, DIFF_HEADS, DIFF_VD)


def sq_relu_mlp(h, w_up, w_down):
    return jnp.square(jax.nn.relu(h @ w_up)) @ w_down


def trunk(x, pos, ret_states, attend, norm_gains, w_ret_in, w_ret_out, w_kv, kv_norm,
          w_q_diff, lam_params, subln_gain, w_o_diff, w_up, w_down):
    new_ret = []
    k_sh = v_sh = None
    for l in range(DEPTH):
        g = norm_gains[l]
        h = rmsnorm(x, g[0])
        if l < N_A_LAYERS:
            y, s = retention_mixer(h, pos, ret_states[l], w_ret_in[l], w_ret_out[l])
            new_ret.append(s)
        else:
            if k_sh is None:
                k_sh, v_sh = shared_kv(x, kv_norm, w_kv)
            i = l - N_A_LAYERS
            y = diff_mixer(h, k_sh, v_sh, attend, w_q_diff[i], lam_params[i], subln_gain[i], w_o_diff[i], lambda_init(l))
        x = x + rmsnorm(y, g[1])
        x = x + rmsnorm(sq_relu_mlp(rmsnorm(x, g[2]), w_up[l], w_down[l]), g[3])
    return x, jnp.stack(new_ret), k_sh, v_sh


def setup_inputs(seed: int = 0) -> dict:
    key = jax.random.key(seed)
    ks = jax.random.split(key, 24)
    f32 = jnp.float32

    def nrm(k, shape, scale):
        return jax.random.normal(k, shape, f32) * scale

    n_pages = PAST_LEN // PAGE_SIZE
    n_used = DEC_BATCH * n_pages
    n_phys = n_used + (n_used + 3) // 4
    page_table = jax.random.permutation(ks[0], n_phys)[:n_used].reshape(DEC_BATCH, n_pages).astype(jnp.int32)
    ret_in_w = 2 * RET_QK_DIM + 2 * RET_V_DIM
    return {
        "x_prompt": nrm(ks[1], (BATCH, SEQ, D_MODEL), 1.0),
        "x_sample": nrm(ks[2], (DEC_BATCH, DEC_SEQ, D_MODEL), 1.0),
        "state_ret": nrm(ks[3], (N_A_LAYERS, DEC_BATCH, RET_HEADS, RET_DK, RET_DV), 1.0),
        "cache_k": nrm(ks[4], (n_phys, PAGE_SIZE, N_QK_HEADS, DIFF_HD), 1.0),
        "cache_v": nrm(ks[5], (n_phys, PAGE_SIZE, DIFF_HEADS, DIFF_VD), 1.0),
        "page_table": page_table,
        "norm_gains": 1.0 + nrm(ks[6], (DEPTH, 4, D_MODEL), 0.02),
        "w_ret_in": nrm(ks[7], (N_A_LAYERS, D_MODEL, ret_in_w), D_MODEL ** -0.5),
        "w_ret_out": nrm(ks[8], (N_A_LAYERS, RET_V_DIM, D_MODEL), RET_V_DIM ** -0.5),
        "w_kv": nrm(ks[9], (D_MODEL, N_QK_HEADS * DIFF_HD + DIFF_HEADS * DIFF_VD), D_MODEL ** -0.5),
        "kv_norm": 1.0 + nrm(ks[10], (D_MODEL,), 0.02),
        "w_q_diff": nrm(ks[11], (N_B_LAYERS, D_MODEL, N_QK_HEADS * DIFF_HD), D_MODEL ** -0.5),
        "lam_params": nrm(ks[12], (N_B_LAYERS, 4, DIFF_HD), 0.1),
        "subln_gain": 1.0 + nrm(ks[13], (N_B_LAYERS, DIFF_VD), 0.02),
        "w_o_diff": nrm(ks[14], (N_B_LAYERS, DIFF_HEADS * DIFF_VD, D_MODEL), (DIFF_HEADS * DIFF_VD) ** -0.5),
        "rel_bias_table": nrm(ks[15], (N_BUCKETS, N_QK_HEADS), 0.5),
        "w_up": nrm(ks[16], (DEPTH, D_MODEL, D_FF), D_MODEL ** -0.5),
        "w_down": nrm(ks[17], (DEPTH, D_FF, D_MODEL), D_FF ** -0.5),
    }


def reference(x_prompt, x_sample, state_ret, cache_k, cache_v, page_table, norm_gains, w_ret_in, w_ret_out,
              w_kv, kv_norm, w_q_diff, lam_params, subln_gain, w_o_diff, rel_bias_table, w_up, w_down):
    params = (norm_gains, w_ret_in, w_ret_out, w_kv, kv_norm, w_q_diff, lam_params, subln_gain, w_o_diff, w_up, w_down)

    pos_p = jnp.arange(x_prompt.shape[1])
    s0 = jnp.zeros((N_A_LAYERS, x_prompt.shape[0], RET_HEADS, RET_DK, RET_DV), x_prompt.dtype)
    y_prompt, ret_prompt, k_prompt, v_prompt = trunk(
        x_prompt, pos_p, s0, functools.partial(attend_prompt, table=rel_bias_table), *params)

    db = page_table.shape[0]
    k_past = cache_k[page_table].reshape(db, -1, N_QK_HEADS, DIFF_HD)
    v_past = cache_v[page_table].reshape(db, -1, DIFF_HEADS, DIFF_VD)
    pos_s = PAST_LEN + jnp.arange(x_sample.shape[1])
    y_sample, ret_sample, k_sample, v_sample = trunk(
        x_sample, pos_s, state_ret,
        functools.partial(attend_sample, table=rel_bias_table, k_past=k_past, v_past=v_past), *params)

    return (y_prompt, y_sample, ret_prompt, k_prompt, v_prompt, ret_sample, k_sample, v_sample)
```

```python
import functools
import math

import jax
import jax.numpy as jnp
from jax import lax
from jax.experimental import pallas as pl
from jax.experimental.pallas import tpu as pltpu

F32 = jnp.float32
BF16 = jnp.bfloat16

D_MODEL = 2048
DEPTH = 4
N_RET_LAYERS = 2
RET_HEADS = 8
RET_DK = 256
RET_DV = 512
RET_QK_DIM = RET_HEADS * RET_DK
RET_V_DIM = RET_HEADS * RET_DV
RET_CHUNK = 128
ROPE_BASE = 10000.0
DIFF_HEADS = 8
DIFF_HD = 128
N_QK_HEADS = 16
DIFF_VD = 256
N_BUCKETS = 32
MAX_DISTANCE = 128
D_FF = 4 * D_MODEL
EPS = 1e-6
NEG_INF = -1e30
PAGE_SIZE = 128

VMEM_LIMIT_BYTES = 52 * 1024 * 1024
ATTN_BLOCK = 256


def _cparams(sem):
    return pltpu.CompilerParams(dimension_semantics=sem, vmem_limit_bytes=VMEM_LIMIT_BYTES)


def _rms(x):
    return x * lax.rsqrt(jnp.mean(x * x, axis=-1, keepdims=True) + EPS)


def _norm_cast_kernel(x_ref, g_ref, o_ref):
    o_ref[...] = (_rms(x_ref[...]) * g_ref[...]).astype(o_ref.dtype)


def _norm_cast(x, g, tm):
    m, d = x.shape
    return pl.pallas_call(
        _norm_cast_kernel,
        out_shape=jax.ShapeDtypeStruct((m, d), BF16),
        grid=(m // tm,),
        in_specs=[pl.BlockSpec((tm, d), lambda i: (i, 0)), pl.BlockSpec((1, d), lambda i: (0, 0))],
        out_specs=pl.BlockSpec((tm, d), lambda i: (i, 0)),
        compiler_params=_cparams(("parallel",)),
        name="norm_cast",
    )(x, g.reshape(1, d))


def _mm_cols_kernel(a_ref, w_ref, *rest, epilogue, scale, n_out, k_tile_start):
    acc = jnp.dot(a_ref[...], w_ref[...], preferred_element_type=F32)
    if epilogue == "rope":
        cos_ref, sin_ref = rest[0], rest[1]
        o_ref = rest[2]
        cos = cos_ref[...]
        sin = sin_ref[...]
        s = jnp.where(pl.program_id(0) >= k_tile_start, scale, 1.0).astype(F32)
        half = RET_DK // 2
        for hd in range(acc.shape[1] // RET_DK):
            x1 = acc[:, hd * RET_DK: hd * RET_DK + half]
            x2 = acc[:, hd * RET_DK + half: (hd + 1) * RET_DK]
            o_ref[:, hd * RET_DK: hd * RET_DK + half] = ((x1 * cos - x2 * sin) * s).astype(o_ref.dtype)
            o_ref[:, hd * RET_DK + half: (hd + 1) * RET_DK] = ((x1 * sin + x2 * cos) * s).astype(o_ref.dtype)
        return
    if epilogue == "relu2":
        r = jnp.maximum(acc, 0.0)
        acc = r * r
    elif epilogue == "scale":
        acc = acc * scale
    for o_ref in rest[:n_out]:
        o_ref[...] = acc.astype(o_ref.dtype)


def _mm_cols(a, w, col_off, n_cols, out_dtypes, tm, tn, epilogue="plain", scale=1.0, rope=None, k_tile_start=0):
    m, k = a.shape
    assert col_off % tn == 0 and n_cols % tn == 0 and m % tm == 0
    off = col_off // tn
    in_specs = [pl.BlockSpec((tm, k), lambda n, i: (i, 0)),
                pl.BlockSpec((k, tn), lambda n, i: (0, n + off))]
    args = [a, w]
    if epilogue == "rope":
        cos, sin = rope
        reps = cos.shape[0] // tm
        in_specs += [pl.BlockSpec((tm, cos.shape[1]), lambda n, i: (i % reps, 0))] * 2
        args += [cos, sin]
    out_shape = [jax.ShapeDtypeStruct((m, n_cols), dt) for dt in out_dtypes]
    out_specs = [pl.BlockSpec((tm, tn), lambda n, i: (i, n)) for _ in out_dtypes]
    outs = pl.pallas_call(
        functools.partial(_mm_cols_kernel, epilogue=epilogue, scale=scale, n_out=len(out_dtypes),
                          k_tile_start=k_tile_start),
        out_shape=out_shape,
        grid=(n_cols // tn, m // tm),
        in_specs=in_specs,
        out_specs=out_specs,
        compiler_params=_cparams(("parallel", "parallel")),
        name="mm_" + epilogue,
    )(*args)
    return outs


def _mm_resid_kernel(a_ref, w_ref, x_ref, gpost_ref, *rest, n_next):
    gnext_refs = rest[:n_next]
    xo_ref = rest[n_next]
    h_refs = rest[n_next + 1: 2 * n_next + 1]
    acc_ref = rest[2 * n_next + 1]
    kk = pl.program_id(1)

    @pl.when(kk == 0)
    def _():
        acc_ref[...] = jnp.zeros_like(acc_ref)

    acc_ref[...] += jnp.dot(a_ref[...], w_ref[...], preferred_element_type=F32)

    @pl.when(kk == pl.num_programs(1) - 1)
    def _():
        xn = x_ref[...] + _rms(acc_ref[...]) * gpost_ref[...]
        xo_ref[...] = xn
        if n_next:
            xh = _rms(xn)
            for g_ref, h_ref in zip(gnext_refs, h_refs):
                h_ref[...] = (xh * g_ref[...]).astype(h_ref.dtype)


def _mm_resid(a, w, x, g_post, g_next, tm, tk):
    m, k = a.shape
    d = w.shape[1]
    n_next = len(g_next)
    row = lambda i, kk: (i, 0)
    const = lambda i, kk: (0, 0)
    outs = pl.pallas_call(
        functools.partial(_mm_resid_kernel, n_next=n_next),
        out_shape=[jax.ShapeDtypeStruct((m, d), F32)] + [jax.ShapeDtypeStruct((m, d), BF16)] * n_next,
        grid=(m // tm, k // tk),
        in_specs=[pl.BlockSpec((tm, tk), lambda i, kk: (i, kk)),
                  pl.BlockSpec((tk, d), lambda i, kk: (kk, 0)),
                  pl.BlockSpec((tm, d), row),
                  pl.BlockSpec((1, d), const)] + [pl.BlockSpec((1, d), const)] * n_next,
        out_specs=[pl.BlockSpec((tm, d), row)] * (1 + n_next),
        scratch_shapes=[pltpu.VMEM((tm, d), F32)],
        compiler_params=_cparams(("parallel", "arbitrary")),
        name="mm_resid",
    )(a, w, x, g_post.reshape(1, d), *[g.reshape(1, d) for g in g_next])
    return outs[0], outs[1:]


def _gate_norm(o, g):
    return (g * jax.nn.sigmoid(g)) * _rms(o)


def _ret_prompt_kernel(ld_ref, q_ref, k_ref, v_ref, g_ref, o_ref, s_out_ref, s_ref, *, n_chunks):
    ld = ld_ref[pl.program_id(1)]
    c = RET_CHUNK
    ri = lax.broadcasted_iota(jnp.int32, (c, c), 0).astype(F32)
    ci = lax.broadcasted_iota(jnp.int32, (c, c), 1).astype(F32)
    rel = ri - ci
    dmask = jnp.where(rel >= 0, jnp.exp(ld * jnp.maximum(rel, 0.0)), 0.0)
    cross_decay = jnp.exp(ld * (lax.broadcasted_iota(jnp.int32, (c, RET_DV), 0).astype(F32) + 1.0))
    k_decay = jnp.exp(ld * (c - 1.0 - lax.broadcasted_iota(jnp.int32, (c, RET_DK), 0).astype(F32)))
    chunk_decay = jnp.exp(jnp.full((1, RET_DV), c, F32) * ld)
    s_ref[...] = jnp.zeros_like(s_ref)

    def body(i, carry):
        r0 = pl.multiple_of(i * c, c)
        q = q_ref[pl.ds(r0, c), :]
        k = k_ref[pl.ds(r0, c), :]
        v = v_ref[pl.ds(r0, c), :]
        g = g_ref[pl.ds(r0, c), :]
        s_prev = s_ref[...]
        scores = lax.dot_general(q, k, (((1,), (1,)), ((), ())), preferred_element_type=F32) * dmask
        o = jnp.dot(scores.astype(BF16), v, preferred_element_type=F32)
        o = o + jnp.dot(q, s_prev.astype(BF16), preferred_element_type=F32) * cross_decay
        kd = (k.astype(F32) * k_decay).astype(BF16)
        s_ref[...] = s_prev * chunk_decay + lax.dot_general(
            kd, v, (((0,), (0,)), ((), ())), preferred_element_type=F32)
        o_ref[pl.ds(r0, c), :] = _gate_norm(o, g).astype(o_ref.dtype)
        return carry

    lax.fori_loop(0, n_chunks, body, 0)
    s_out_ref[0, 0] = s_ref[...]


def _ret_prompt(log_decay, qk, v, g, batch, seq):
    m = batch * seq
    return pl.pallas_call(
        functools.partial(_ret_prompt_kernel, n_chunks=seq // RET_CHUNK),
        out_shape=[jax.ShapeDtypeStruct((m, RET_V_DIM), BF16),
                   jax.ShapeDtypeStruct((batch, RET_HEADS, RET_DK, RET_DV), F32)],
        grid=(batch, RET_HEADS),
        in_specs=[pl.BlockSpec(memory_space=pltpu.SMEM),
                  pl.BlockSpec((seq, RET_DK), lambda b, h: (b, h)),
                  pl.BlockSpec((seq, RET_DK), lambda b, h: (b, RET_HEADS + h)),
                  pl.BlockSpec((seq, RET_DV), lambda b, h: (b, h)),
                  pl.BlockSpec((seq, RET_DV), lambda b, h: (b, h))],
        out_specs=[pl.BlockSpec((seq, RET_DV), lambda b, h: (b, h)),
                   pl.BlockSpec((1, 1, RET_DK, RET_DV), lambda b, h: (b, h, 0, 0))],
        scratch_shapes=[pltpu.VMEM((RET_DK, RET_DV), F32)],
        compiler_params=_cparams(("parallel", "parallel")),
        name="ret_prompt",
    )(log_decay, qk, qk, v, g)


def _ret_sample_kernel(ld_ref, q_ref, k_ref, v_ref, g_ref, s0_ref, o_ref, s_out_ref, *, batch, seq):
    ld = ld_ref[pl.program_id(0)]
    rows = batch * seq
    shift = seq.bit_length() - 1
    ri = lax.broadcasted_iota(jnp.int32, (rows, rows), 0)
    ci = lax.broadcasted_iota(jnp.int32, (rows, rows), 1)
    rel = ((ri & (seq - 1)) - (ci & (seq - 1))).astype(F32)
    same = (ri >> shift) == (ci >> shift)
    dmask = jnp.where(same, jnp.where(rel >= 0, jnp.exp(ld * jnp.maximum(rel, 0.0)), 0.0), 0.0)
    rv = lax.broadcasted_iota(jnp.int32, (rows, RET_DV), 0)
    rk = lax.broadcasted_iota(jnp.int32, (rows, RET_DK), 0)
    cross_decay = jnp.exp(ld * ((rv & (seq - 1)).astype(F32) + 1.0))
    k_decay = jnp.exp(ld * (seq - 1.0 - (rk & (seq - 1)).astype(F32)))
    chunk_decay = jnp.exp(jnp.full((1, RET_DV), seq, F32) * ld)

    q = q_ref[...]
    k = k_ref[...]
    v = v_ref[...]
    scores = lax.dot_general(q, k, (((1,), (1,)), ((), ())), preferred_element_type=F32) * dmask
    o = jnp.dot(scores.astype(BF16), v, preferred_element_type=F32)
    kd = k.astype(F32) * k_decay
    for b in range(batch):
        s_prev = s0_ref[b, 0]
        ob = jnp.dot(q, s_prev.astype(BF16), preferred_element_type=F32) * cross_decay
        o = o + jnp.where((rv >> shift) == b, ob, 0.0)
        kb = jnp.where((rk >> shift) == b, kd, 0.0).astype(BF16)
        s_out_ref[b, 0] = s_prev * chunk_decay + lax.dot_general(
            kb, v, (((0,), (0,)), ((), ())), preferred_element_type=F32)
    o_ref[...] = _gate_norm(o, g_ref[...]).astype(o_ref.dtype)


def _ret_sample(log_decay, qk, v, g, s0, batch, seq):
    rows = batch * seq
    return pl.pallas_call(
        functools.partial(_ret_sample_kernel, batch=batch, seq=seq),
        out_shape=[jax.ShapeDtypeStruct((rows, RET_V_DIM), BF16),
                   jax.ShapeDtypeStruct((batch, RET_HEADS, RET_DK, RET_DV), F32)],
        grid=(RET_HEADS,),
        in_specs=[pl.BlockSpec(memory_space=pltpu.SMEM),
                  pl.BlockSpec((rows, RET_DK), lambda h: (0, h)),
                  pl.BlockSpec((rows, RET_DK), lambda h: (0, RET_HEADS + h)),
                  pl.BlockSpec((rows, RET_DV), lambda h: (0, h)),
                  pl.BlockSpec((rows, RET_DV), lambda h: (0, h)),
                  pl.BlockSpec((batch, 1, RET_DK, RET_DV), lambda h: (0, h, 0, 0))],
        out_specs=[pl.BlockSpec((rows, RET_DV), lambda h: (0, h)),
                   pl.BlockSpec((batch, 1, RET_DK, RET_DV), lambda h: (0, h, 0, 0))],
        compiler_params=_cparams(("parallel",)),
        name="ret_sample",
    )(log_decay, qk, qk, v, g, s0)


def _softmax_step(j, s, v, m_ref, l_ref, acc_ref):
    m_prev = m_ref[j]
    m_new = jnp.maximum(m_prev, jnp.max(s, axis=-1, keepdims=True))
    alpha = jnp.exp(m_prev - m_new)
    p = jnp.exp(s - m_new[:, :1])
    l_ref[j] = alpha * l_ref[j] + jnp.sum(p, axis=-1, keepdims=True)
    acc_ref[j] = acc_ref[j] * alpha[:, :1] + jnp.dot(p.astype(BF16), v, preferred_element_type=F32)
    m_ref[j] = m_new


def _attn_prompt_kernel(scal_ref, q_ref, k_ref, v_ref, bd_ref, bs_ref, subg_ref, o_ref,
                        m_ref, l_ref, acc_ref, *, out_scale):
    h = pl.program_id(1)
    i = pl.program_id(2)
    tb = ATTN_BLOCK
    m_ref[...] = jnp.full_like(m_ref, NEG_INF)
    l_ref[...] = jnp.zeros_like(l_ref)
    acc_ref[...] = jnp.zeros_like(acc_ref)
    q = q_ref[...]

    def block(kb, bias):
        r0 = pl.multiple_of(kb * tb, tb)
        kblk = k_ref[pl.ds(r0, tb), :]
        vblk = v_ref[pl.ds(r0, tb), :]
        for j in range(2):
            s = lax.dot_general(q[:, j * DIFF_HD:(j + 1) * DIFF_HD], kblk[:, j * DIFF_HD:(j + 1) * DIFF_HD],
                                (((1,), (1,)), ((), ())), preferred_element_type=F32)
            _softmax_step(j, s + bias(j), vblk, m_ref, l_ref, acc_ref)

    def far(kb, carry):
        block(kb, lambda j: scal_ref[1 + 2 * h + j])
        return carry

    lax.fori_loop(0, i - 1, far, 0)

    @pl.when(i >= 1)
    def _():
        block(i - 1, lambda j: bs_ref[j])

    block(i, lambda j: bd_ref[j])

    lam = scal_ref[0]
    o = acc_ref[0] / l_ref[0][:, :1] - lam * (acc_ref[1] / l_ref[1][:, :1])
    o_ref[...] = (_rms(o) * subg_ref[...] * out_scale).astype(o_ref.dtype)


def _attn_prompt(scal, q, k, v, bias_diag, bias_sub, sub_g, batch, seq, out_scale):
    m = batch * seq
    tb = ATTN_BLOCK
    nq = seq // tb
    w = 2 * DIFF_HD
    return pl.pallas_call(
        functools.partial(_attn_prompt_kernel, out_scale=out_scale),
        out_shape=jax.ShapeDtypeStruct((m, DIFF_HEADS * DIFF_VD), BF16),
        grid=(batch, DIFF_HEADS, nq),
        in_specs=[pl.BlockSpec(memory_space=pltpu.SMEM),
                  pl.BlockSpec((tb, w), lambda b, h, i: (b * nq + i, h)),
                  pl.BlockSpec((seq, w), lambda b, h, i: (b, h)),
                  pl.BlockSpec((seq, DIFF_VD), lambda b, h, i: (b, h)),
                  pl.BlockSpec((2, tb, tb), lambda b, h, i: (h, 0, 0)),
                  pl.BlockSpec((2, tb, tb), lambda b, h, i: (h, 0, 0)),
                  pl.BlockSpec((1, DIFF_VD), lambda b, h, i: (0, 0))],
        out_specs=pl.BlockSpec((tb, DIFF_VD), lambda b, h, i: (b * nq + i, h)),
        scratch_shapes=[pltpu.VMEM((2, tb, 128), F32), pltpu.VMEM((2, tb, 128), F32),
                        pltpu.VMEM((2, tb, DIFF_VD), F32)],
        compiler_params=_cparams(("parallel", "parallel", "arbitrary")),
        name="attn_prompt",
    )(scal, q, k, v, bias_diag, bias_sub, sub_g.reshape(1, DIFF_VD))


def _attn_sample_kernel(pt_ref, lam_ref, q_ref, kc_ref, vc_ref, kn_ref, vn_ref, bfar_ref, blast_ref, bnew_ref,
                        subg_ref, o_ref, m_ref, l_ref, acc_ref, *, out_scale, seq):
    p = pl.program_id(1)
    n_pages = pl.num_programs(1)

    @pl.when(p == 0)
    def _():
        m_ref[...] = jnp.full_like(m_ref, NEG_INF)
        l_ref[...] = jnp.zeros_like(l_ref)
        acc_ref[...] = jnp.zeros_like(acc_ref)

    q = q_ref[0]

    def keys(kt, vt, bias):
        s = lax.dot_general(q, kt.astype(BF16), (((1,), (1,)), ((), ())), preferred_element_type=F32)
        _softmax_step(0, s + bias, vt.astype(BF16), m_ref, l_ref, acc_ref)

    @pl.when(p < n_pages - 1)
    def _():
        keys(kc_ref[0], vc_ref[0], bfar_ref[...])

    @pl.when(p == n_pages - 1)
    def _():
        keys(kc_ref[0], vc_ref[0], blast_ref[...])
        keys(kn_ref[0], vn_ref[0], bnew_ref[...])
        a = acc_ref[0] / l_ref[0][:, :1]
        half = a.shape[0] // 2
        d = a[:half] - lam_ref[0] * a[half:]
        rh = lax.broadcasted_iota(jnp.int32, d.shape, 0) & (DIFF_HEADS - 1)
        ch = lax.broadcasted_iota(jnp.int32, d.shape, 1) >> (DIFF_VD.bit_length() - 1)
        d = jnp.where(rh == ch, d, 0.0)
        d = d * lax.rsqrt(jnp.sum(d * d, axis=-1, keepdims=True) / DIFF_VD + EPS)
        d = d * subg_ref[...] * out_scale
        o_ref[0] = jnp.sum(d.reshape(seq, DIFF_HEADS, d.shape[1]), axis=1).astype(o_ref.dtype)


def _attn_sample(page_table, lam, qbd, cache_k, cache_v, k_new, v_new, b_far, b_last, b_new, sub_g_tiled,
                 out_scale, seq):
    batch, n_pages = page_table.shape
    rows = qbd.shape[1]
    wk = cache_k.shape[2]
    wv = cache_v.shape[2]
    page = cache_k.shape[1]
    const2 = lambda b, p, pt: (0, 0)
    per_b = lambda b, p, pt: (b, 0, 0)
    grid_spec = pltpu.PrefetchScalarGridSpec(
        num_scalar_prefetch=1,
        grid=(batch, n_pages),
        in_specs=[pl.BlockSpec(memory_space=pltpu.SMEM),
                  pl.BlockSpec((1, rows, wk), per_b),
                  pl.BlockSpec((1, page, wk), lambda b, p, pt: (pt[b, p], 0, 0)),
                  pl.BlockSpec((1, page, wv), lambda b, p, pt: (pt[b, p], 0, 0)),
                  pl.BlockSpec((1, page, wk), per_b),
                  pl.BlockSpec((1, page, wv), per_b),
                  pl.BlockSpec((rows, page), const2),
                  pl.BlockSpec((rows, page), const2),
                  pl.BlockSpec((rows, page), const2),
                  pl.BlockSpec((1, wv), const2)],
        out_specs=pl.BlockSpec((1, seq, wv), per_b),
        scratch_shapes=[pltpu.VMEM((1, rows, 128), F32), pltpu.VMEM((1, rows, 128), F32),
                        pltpu.VMEM((1, rows, wv), F32)],
    )
    return pl.pallas_call(
        functools.partial(_attn_sample_kernel, out_scale=out_scale, seq=seq),
        out_shape=jax.ShapeDtypeStruct((batch, seq, wv), F32),
        grid_spec=grid_spec,
        compiler_params=_cparams(("parallel", "arbitrary")),
        name="attn_sample",
    )(page_table, lam, qbd, cache_k, cache_v, k_new, v_new, b_far, b_last, b_new, sub_g_tiled)


def _rope_tables(pos):
    half = RET_DK // 2
    inv = ROPE_BASE ** (-jnp.arange(half, dtype=F32) / half)
    ang = pos.astype(F32)[:, None] * inv[None, :]
    return jnp.cos(ang), jnp.sin(ang)


def _log_decay():
    return jnp.log1p(-jnp.exp2(-5.0 - jnp.arange(RET_HEADS, dtype=F32)))


def _bias_by_distance(n, table):
    max_exact = N_BUCKETS // 2
    nf = jnp.maximum(n, max_exact).astype(F32)
    large = max_exact + (jnp.log(nf / max_exact) / math.log(MAX_DISTANCE / max_exact)
                         * (N_BUCKETS - max_exact)).astype(jnp.int32)
    bucket = jnp.where(n < max_exact, n, jnp.minimum(large, N_BUCKETS - 1))
    return table[bucket].astype(F32)


def _lambda_init(layer_idx):
    return 0.8 - 0.6 * math.exp(-0.3 * layer_idx)


def _lam(lam_p):
    lp = lam_p.astype(F32)
    return jnp.exp(jnp.sum(lp[0] * lp[1])) - jnp.exp(jnp.sum(lp[2] * lp[3]))


def _trunk(x, batch, seq, tm, rope, mixers, wts):
    (norm_gains, w_ret_in, w_ret_out, w_kv, kv_norm, w_q, w_o, w_up, w_down) = wts
    tn = 1024
    h = _norm_cast(x, norm_gains[0, 0], tm)
    ret_states = []
    k32 = v32 = None
    for l in range(DEPTH):
        g = norm_gains[l]
        if l < N_RET_LAYERS:
            w_in = w_ret_in[l]
            (qk,) = _mm_cols(h, w_in, 0, 2 * RET_QK_DIM, [BF16], tm, tn, epilogue="rope",
                             scale=RET_DK ** -0.5, rope=rope, k_tile_start=RET_QK_DIM // tn)
            (v,) = _mm_cols(h, w_in, 2 * RET_QK_DIM, RET_V_DIM, [BF16], tm, tn)
            (gate,) = _mm_cols(h, w_in, 2 * RET_QK_DIM + RET_V_DIM, RET_V_DIM, [F32], tm, tn)
            o, s = mixers["ret"](l, qk, v, gate)
            ret_states.append(s)
            x, (h,) = _mm_resid(o, w_ret_out[l], x, g[1], [g[2]], min(tm, 512), 1024)
        else:
            i = l - N_RET_LAYERS
            (q,) = _mm_cols(h, w_q[i], 0, N_QK_HEADS * DIFF_HD, [BF16], tm, tn, epilogue="scale",
                            scale=DIFF_HD ** -0.5)
            o = mixers["attn"](i, l, q, k16, v16)
            x, (h,) = _mm_resid(o, w_o[i], x, g[1], [g[2]], min(tm, 512), 1024)
        (u,) = _mm_cols(h, w_up[l], 0, D_FF, [BF16], tm, tn, epilogue="relu2")
        if l == N_RET_LAYERS - 1:
            x, (h, hkv) = _mm_resid(u, w_down[l], x, g[3], [norm_gains[l + 1, 0], kv_norm], min(tm, 512), 1024)
            k32, k16 = _mm_cols(hkv, w_kv, 0, N_QK_HEADS * DIFF_HD, [F32, BF16], tm, tn)
            v32, v16 = _mm_cols(hkv, w_kv, N_QK_HEADS * DIFF_HD, DIFF_HEADS * DIFF_VD, [F32, BF16], tm, tn)
            k16, v16 = mixers["kv"](k32, v32, k16, v16)
        elif l < DEPTH - 1:
            x, (h,) = _mm_resid(u, w_down[l], x, g[3], [norm_gains[l + 1, 0]], min(tm, 512), 1024)
        else:
            x, _ = _mm_resid(u, w_down[l], x, g[3], [], min(tm, 512), 1024)
    return x, jnp.stack(ret_states), k32, v32


def kernel(x_prompt, x_sample, state_ret, cache_k, cache_v, page_table, norm_gains, w_ret_in, w_ret_out,
           w_kv, kv_norm, w_q_diff, lam_params, subln_gain, w_o_diff, rel_bias_table, w_up, w_down):
    batch, seq, d = x_prompt.shape
    dec_batch, dec_seq, _ = x_sample.shape
    n_pages = page_table.shape[1]
    past_len = n_pages * PAGE_SIZE
    tb = ATTN_BLOCK

    wts = (norm_gains, w_ret_in.astype(BF16), w_ret_out.astype(BF16), w_kv.astype(BF16), kv_norm,
           w_q_diff.astype(BF16), w_o_diff.astype(BF16), w_up.astype(BF16), w_down.astype(BF16))
    log_decay = _log_decay()
    lams = [_lam(lam_params[i]) + _lambda_init(N_RET_LAYERS + i) for i in range(DEPTH - N_RET_LAYERS)]
    out_scales = [1.0 - _lambda_init(N_RET_LAYERS + i) for i in range(DEPTH - N_RET_LAYERS)]

    dist = jnp.arange(tb)[:, None] - jnp.arange(tb)[None, :]
    bias_diag = jnp.where(dist[None] >= 0,
                          jnp.moveaxis(_bias_by_distance(jnp.maximum(dist, 0), rel_bias_table), -1, 0), NEG_INF)
    bias_sub = jnp.moveaxis(_bias_by_distance(dist + tb, rel_bias_table), -1, 0)
    bias_far = _bias_by_distance(jnp.full((), 2 * tb, jnp.int32), rel_bias_table)

    def p_ret(l, qk, v, gate):
        return _ret_prompt(log_decay, qk, v, gate, batch, seq)

    def p_attn(i, l, q, k16, v16):
        scal = jnp.concatenate([lams[i][None], bias_far]).astype(F32)
        return _attn_prompt(scal, q, k16, v16, bias_diag, bias_sub, subln_gain[i], batch, seq, out_scales[i])

    y_p, ret_p, k_p, v_p = _trunk(
        x_prompt.reshape(batch * seq, d), batch, seq, math.gcd(seq, 1024), _rope_tables(jnp.arange(seq)),
        {"ret": p_ret, "attn": p_attn, "kv": lambda k32, v32, k16, v16: (k16, v16)}, wts)

    rows_s = dec_batch * dec_seq
    cos_s, sin_s = _rope_tables(past_len + jnp.arange(dec_seq))
    rope_s = (jnp.tile(cos_s, (dec_batch, 1)), jnp.tile(sin_s, (dec_batch, 1)))
    ck = cache_k.reshape(cache_k.shape[0], PAGE_SIZE, N_QK_HEADS * DIFF_HD)
    cv = cache_v.reshape(cache_v.shape[0], PAGE_SIZE, DIFF_HEADS * DIFF_VD)
    n_rows = 2 * dec_seq * DIFF_HEADS
    r = jnp.arange(n_rows)
    row_t = (r // DIFF_HEADS) % dec_seq
    row_map = 2 * (r % DIFF_HEADS) + r // (dec_seq * DIFF_HEADS)
    cols = jnp.arange(PAGE_SIZE)
    pick = lambda bias: jnp.take_along_axis(bias, row_map[:, None, None], axis=2)[:, :, 0]
    b_far = jnp.broadcast_to(_bias_by_distance(jnp.full((), past_len, jnp.int32), rel_bias_table)[row_map][:, None],
                             (n_rows, PAGE_SIZE))
    d_last = (past_len + row_t)[:, None] - (past_len - PAGE_SIZE + cols)[None, :]
    b_last = pick(_bias_by_distance(d_last, rel_bias_table))
    d_new = row_t[:, None] - cols[None, :]
    b_new = jnp.where(d_new >= 0, pick(_bias_by_distance(jnp.maximum(d_new, 0), rel_bias_table)), NEG_INF)
    map_onehot = (row_map[:, None] == jnp.arange(N_QK_HEADS)[None, :])

    def s_ret(l, qk, v, gate):
        return _ret_sample(log_decay, qk, v, gate, state_ret[l], dec_batch, dec_seq)

    new_kv = {}

    def s_kv(k32, v32, k16, v16):
        pad = lambda a: jnp.pad(a.reshape(dec_batch, dec_seq, -1), ((0, 0), (0, PAGE_SIZE - dec_seq), (0, 0)))
        new_kv["k"], new_kv["v"] = pad(k16), pad(v16)
        return k16, v16

    def s_attn(i, l, q, k16, v16):
        qr = q.reshape(dec_batch, dec_seq, DIFF_HEADS, 2, DIFF_HD).transpose(0, 3, 1, 2, 4)
        qr = qr.reshape(dec_batch, n_rows, 1, DIFF_HD)
        qbd = jnp.where(map_onehot[None, :, :, None], qr, 0).reshape(dec_batch, n_rows, N_QK_HEADS * DIFF_HD)
        o = _attn_sample(page_table, lams[i][None].astype(F32), qbd, ck, cv, new_kv["k"], new_kv["v"],
                         b_far, b_last, b_new, jnp.tile(subln_gain[i], DIFF_HEADS).reshape(1, -1),
                         out_scales[i], dec_seq)
        return o.reshape(rows_s, DIFF_HEADS * DIFF_VD).astype(BF16)

    y_s, ret_s, k_s, v_s = _trunk(
        x_sample.reshape(rows_s, d), dec_batch, dec_seq, rows_s, rope_s,
        {"ret": s_ret, "attn": s_attn, "kv": s_kv}, wts)

    return (y_p.reshape(batch, seq, d), y_s.reshape(dec_batch, dec_seq, d), ret_p,
            k_p.reshape(batch, seq, N_QK_HEADS, DIFF_HD), v_p.reshape(batch, seq, DIFF_HEADS, DIFF_VD),
            ret_s, k_s.reshape(dec_batch, dec_seq, N_QK_HEADS, DIFF_HD),
            v_s.reshape(dec_batch, dec_seq, DIFF_HEADS, DIFF_VD))
```

```python
import functools
import math

import jax
import jax.numpy as jnp
from jax import lax
from jax.experimental import pallas as pl
from jax.experimental.pallas import tpu as pltpu

F32 = jnp.float32
BF16 = jnp.bfloat16

D_MODEL = 2048
DEPTH = 4
N_RET_LAYERS = 2
RET_HEADS = 8
RET_DK = 256
RET_DV = 512
RET_QK_DIM = RET_HEADS * RET_DK
RET_V_DIM = RET_HEADS * RET_DV
RET_CHUNK = 128
ROPE_BASE = 10000.0
DIFF_HEADS = 8
DIFF_HD = 128
N_QK_HEADS = 16
DIFF_VD = 256
N_BUCKETS = 32
MAX_DISTANCE = 128
D_FF = 4 * D_MODEL
EPS = 1e-6
NEG_INF = -1e30
PAGE_SIZE = 128

VMEM_LIMIT_BYTES = 52 * 1024 * 1024
LANES = 128
ATTN_Q_BLOCK = 512
ATTN_K_BLOCK = 256
ATTN_NEAR = ATTN_Q_BLOCK // ATTN_K_BLOCK + 1
ATTN_BASE_W = 1024


def _cparams(sem):
    return pltpu.CompilerParams(dimension_semantics=sem, vmem_limit_bytes=VMEM_LIMIT_BYTES)


def _rms(x):
    return x * lax.rsqrt(jnp.mean(x * x, axis=-1, keepdims=True) + EPS)


def _norm_cast_kernel(x_ref, g_ref, o_ref):
    o_ref[...] = (_rms(x_ref[...]) * g_ref[...]).astype(o_ref.dtype)


def _norm_cast(x, g, tm):
    m, d = x.shape
    return pl.pallas_call(
        _norm_cast_kernel,
        out_shape=jax.ShapeDtypeStruct((m, d), BF16),
        grid=(m // tm,),
        in_specs=[pl.BlockSpec((tm, d), lambda i: (i, 0)), pl.BlockSpec((1, d), lambda i: (0, 0))],
        out_specs=pl.BlockSpec((tm, d), lambda i: (i, 0)),
        compiler_params=_cparams(("parallel",)),
        name="norm_cast",
    )(x, g.reshape(1, d))


def _mm_cols_kernel(a_ref, w_ref, *rest, epilogue, scale, n_out, k_tile_start):
    wb_ref = rest[-1]

    @pl.when(pl.program_id(1) == 0)
    def _():
        wb_ref[...] = w_ref[...].astype(BF16)

    acc = jnp.dot(a_ref[...], wb_ref[...], preferred_element_type=F32)
    if epilogue == "rope":
        cos_ref, sin_ref = rest[0], rest[1]
        o_ref = rest[2]
        cos = cos_ref[...]
        sin = sin_ref[...]
        s = jnp.where(pl.program_id(0) >= k_tile_start, scale, 1.0).astype(F32)
        half = RET_DK // 2
        for hd in range(acc.shape[1] // RET_DK):
            x1 = acc[:, hd * RET_DK: hd * RET_DK + half]
            x2 = acc[:, hd * RET_DK + half: (hd + 1) * RET_DK]
            o_ref[:, hd * RET_DK: hd * RET_DK + half] = ((x1 * cos - x2 * sin) * s).astype(o_ref.dtype)
            o_ref[:, hd * RET_DK + half: (hd + 1) * RET_DK] = ((x1 * sin + x2 * cos) * s).astype(o_ref.dtype)
        return
    if epilogue == "relu2":
        r = jnp.maximum(acc, 0.0)
        acc = r * r
    elif epilogue == "scale":
        acc = acc * scale
    for o_ref in rest[:n_out]:
        o_ref[...] = acc.astype(o_ref.dtype)


def _mm_cols(a, w, col_off, n_cols, out_dtypes, tm, tn, epilogue="plain", scale=1.0, rope=None, k_tile_start=0):
    m, k = a.shape
    assert col_off % tn == 0 and n_cols % tn == 0 and m % tm == 0
    off = col_off // tn
    in_specs = [pl.BlockSpec((tm, k), lambda n, i: (i, 0)),
                pl.BlockSpec((k, tn), lambda n, i: (0, n + off))]
    args = [a, w]
    if epilogue == "rope":
        cos, sin = rope
        reps = cos.shape[0] // tm
        in_specs += [pl.BlockSpec((tm, cos.shape[1]), lambda n, i: (i % reps, 0))] * 2
        args += [cos, sin]
    out_shape = [jax.ShapeDtypeStruct((m, n_cols), dt) for dt in out_dtypes]
    out_specs = [pl.BlockSpec((tm, tn), lambda n, i: (i, n)) for _ in out_dtypes]
    outs = pl.pallas_call(
        functools.partial(_mm_cols_kernel, epilogue=epilogue, scale=scale, n_out=len(out_dtypes),
                          k_tile_start=k_tile_start),
        out_shape=out_shape,
        grid=(n_cols // tn, m // tm),
        in_specs=in_specs,
        out_specs=out_specs,
        scratch_shapes=[pltpu.VMEM((k, tn), BF16)],
        compiler_params=_cparams(("parallel", "arbitrary")),
        name="mm_" + epilogue,
    )(*args)
    return outs


def _mm_resid_kernel(a_ref, w_ref, x_ref, gpost_ref, *rest, n_next):
    gnext_refs = rest[:n_next]
    xo_ref = rest[n_next]
    h_refs = rest[n_next + 1: 2 * n_next + 1]
    acc_ref = rest[2 * n_next + 1]
    kk = pl.program_id(1)

    @pl.when(kk == 0)
    def _():
        acc_ref[...] = jnp.zeros_like(acc_ref)

    acc_ref[...] += jnp.dot(a_ref[...], w_ref[...], preferred_element_type=F32)

    @pl.when(kk == pl.num_programs(1) - 1)
    def _():
        xn = x_ref[...] + _rms(acc_ref[...]) * gpost_ref[...]
        xo_ref[...] = xn
        if n_next:
            xh = _rms(xn)
            for g_ref, h_ref in zip(gnext_refs, h_refs):
                h_ref[...] = (xh * g_ref[...]).astype(h_ref.dtype)


def _mm_resid(a, w, x, g_post, g_next, tm, tk):
    m, k = a.shape
    d = w.shape[1]
    n_next = len(g_next)
    row = lambda i, kk: (i, 0)
    const = lambda i, kk: (0, 0)
    outs = pl.pallas_call(
        functools.partial(_mm_resid_kernel, n_next=n_next),
        out_shape=[jax.ShapeDtypeStruct((m, d), F32)] + [jax.ShapeDtypeStruct((m, d), BF16)] * n_next,
        grid=(m // tm, k // tk),
        in_specs=[pl.BlockSpec((tm, tk), lambda i, kk: (i, kk)),
                  pl.BlockSpec((tk, d), lambda i, kk: (kk, 0)),
                  pl.BlockSpec((tm, d), row),
                  pl.BlockSpec((1, d), const)] + [pl.BlockSpec((1, d), const)] * n_next,
        out_specs=[pl.BlockSpec((tm, d), row)] * (1 + n_next),
        scratch_shapes=[pltpu.VMEM((tm, d), F32)],
        compiler_params=_cparams(("parallel", "arbitrary")),
        name="mm_resid",
    )(a, w, x, g_post.reshape(1, d), *[g.reshape(1, d) for g in g_next])
    return outs[0], outs[1:]


def _gate_norm(o, g):
    return (g * jax.nn.sigmoid(g)) * _rms(o)


def _ret_prompt_kernel(ld_ref, q_ref, k_ref, v_ref, g_ref, o_ref, s_out_ref, s_ref, *, n_chunks):
    ld = ld_ref[pl.program_id(1)]
    c = RET_CHUNK
    ri = lax.broadcasted_iota(jnp.int32, (c, c), 0).astype(F32)
    ci = lax.broadcasted_iota(jnp.int32, (c, c), 1).astype(F32)
    rel = ri - ci
    dmask = jnp.where(rel >= 0, jnp.exp(ld * jnp.maximum(rel, 0.0)), 0.0)
    cross_decay = jnp.exp(ld * (lax.broadcasted_iota(jnp.int32, (c, RET_DV), 0).astype(F32) + 1.0))
    k_decay = jnp.exp(ld * (c - 1.0 - lax.broadcasted_iota(jnp.int32, (c, RET_DK), 0).astype(F32)))
    chunk_decay = jnp.exp(jnp.full((1, RET_DV), c, F32) * ld)
    s_ref[...] = jnp.zeros_like(s_ref)

    def body(i, carry):
        r0 = pl.multiple_of(i * c, c)
        q = q_ref[pl.ds(r0, c), :]
        k = k_ref[pl.ds(r0, c), :]
        v = v_ref[pl.ds(r0, c), :]
        g = g_ref[pl.ds(r0, c), :]
        s_prev = s_ref[...]
        scores = lax.dot_general(q, k, (((1,), (1,)), ((), ())), preferred_element_type=F32) * dmask
        o = jnp.dot(scores.astype(BF16), v, preferred_element_type=F32)
        o = o + jnp.dot(q, s_prev.astype(BF16), preferred_element_type=F32) * cross_decay
        kd = (k.astype(F32) * k_decay).astype(BF16)
        s_ref[...] = s_prev * chunk_decay + lax.dot_general(
            kd, v, (((0,), (0,)), ((), ())), preferred_element_type=F32)
        o_ref[pl.ds(r0, c), :] = _gate_norm(o, g).astype(o_ref.dtype)
        return carry

    lax.fori_loop(0, n_chunks, body, 0)
    s_out_ref[0, 0] = s_ref[...]


def _ret_prompt(log_decay, qk, v, g, batch, seq):
    m = batch * seq
    return pl.pallas_call(
        functools.partial(_ret_prompt_kernel, n_chunks=seq // RET_CHUNK),
        out_shape=[jax.ShapeDtypeStruct((m, RET_V_DIM), BF16),
                   jax.ShapeDtypeStruct((batch, RET_HEADS, RET_DK, RET_DV), F32)],
        grid=(batch, RET_HEADS),
        in_specs=[pl.BlockSpec(memory_space=pltpu.SMEM),
                  pl.BlockSpec((seq, RET_DK), lambda b, h: (b, h)),
                  pl.BlockSpec((seq, RET_DK), lambda b, h: (b, RET_HEADS + h)),
                  pl.BlockSpec((seq, RET_DV), lambda b, h: (b, h)),
                  pl.BlockSpec((seq, RET_DV), lambda b, h: (b, h))],
        out_specs=[pl.BlockSpec((seq, RET_DV), lambda b, h: (b, h)),
                   pl.BlockSpec((1, 1, RET_DK, RET_DV), lambda b, h: (b, h, 0, 0))],
        scratch_shapes=[pltpu.VMEM((RET_DK, RET_DV), F32)],
        compiler_params=_cparams(("parallel", "parallel")),
        name="ret_prompt",
    )(log_decay, qk, qk, v, g)


def _ret_sample_kernel(ld_ref, q_ref, k_ref, v_ref, g_ref, s0_ref, o_ref, s_out_ref, *, batch, seq):
    ld = ld_ref[pl.program_id(0)]
    rows = batch * seq
    shift = seq.bit_length() - 1
    ri = lax.broadcasted_iota(jnp.int32, (rows, rows), 0)
    ci = lax.broadcasted_iota(jnp.int32, (rows, rows), 1)
    rel = ((ri & (seq - 1)) - (ci & (seq - 1))).astype(F32)
    same = (ri >> shift) == (ci >> shift)
    dmask = jnp.where(same, jnp.where(rel >= 0, jnp.exp(ld * jnp.maximum(rel, 0.0)), 0.0), 0.0)
    rv = lax.broadcasted_iota(jnp.int32, (rows, RET_DV), 0)
    rk = lax.broadcasted_iota(jnp.int32, (rows, RET_DK), 0)
    cross_decay = jnp.exp(ld * ((rv & (seq - 1)).astype(F32) + 1.0))
    k_decay = jnp.exp(ld * (seq - 1.0 - (rk & (seq - 1)).astype(F32)))
    chunk_decay = jnp.exp(jnp.full((1, RET_DV), seq, F32) * ld)

    q = q_ref[...]
    k = k_ref[...]
    v = v_ref[...]
    scores = lax.dot_general(q, k, (((1,), (1,)), ((), ())), preferred_element_type=F32) * dmask
    o = jnp.dot(scores.astype(BF16), v, preferred_element_type=F32)
    kd = k.astype(F32) * k_decay
    for b in range(batch):
        s_prev = s0_ref[b, 0]
        ob = jnp.dot(q, s_prev.astype(BF16), preferred_element_type=F32) * cross_decay
        o = o + jnp.where((rv >> shift) == b, ob, 0.0)
        kb = jnp.where((rk >> shift) == b, kd, 0.0).astype(BF16)
        s_out_ref[b, 0] = s_prev * chunk_decay + lax.dot_general(
            kb, v, (((0,), (0,)), ((), ())), preferred_element_type=F32)
    o_ref[...] = _gate_norm(o, g_ref[...]).astype(o_ref.dtype)


def _ret_sample(log_decay, qk, v, g, s0, batch, seq):
    rows = batch * seq
    return pl.pallas_call(
        functools.partial(_ret_sample_kernel, batch=batch, seq=seq),
        out_shape=[jax.ShapeDtypeStruct((rows, RET_V_DIM), BF16),
                   jax.ShapeDtypeStruct((batch, RET_HEADS, RET_DK, RET_DV), F32)],
        grid=(RET_HEADS,),
        in_specs=[pl.BlockSpec(memory_space=pltpu.SMEM),
                  pl.BlockSpec((rows, RET_DK), lambda h: (0, h)),
                  pl.BlockSpec((rows, RET_DK), lambda h: (0, RET_HEADS + h)),
                  pl.BlockSpec((rows, RET_DV), lambda h: (0, h)),
                  pl.BlockSpec((rows, RET_DV), lambda h: (0, h)),
                  pl.BlockSpec((batch, 1, RET_DK, RET_DV), lambda h: (0, h, 0, 0))],
        out_specs=[pl.BlockSpec((rows, RET_DV), lambda h: (0, h)),
                   pl.BlockSpec((batch, 1, RET_DK, RET_DV), lambda h: (0, h, 0, 0))],
        compiler_params=_cparams(("parallel",)),
        name="ret_sample",
    )(log_decay, qk, qk, v, g, s0)


def _lane_tile(x, width):
    return x if width == LANES else jnp.concatenate([x] * (width // LANES), axis=1)


def _softmax_step(j, s, v, m_ref, l_ref, acc_ref):
    m_prev = m_ref[j]
    m_new = jnp.maximum(m_prev, jnp.max(s, axis=-1, keepdims=True))
    alpha = jnp.exp(m_prev - m_new)
    p = jnp.exp(s - _lane_tile(m_new, s.shape[1]))
    l_ref[j] = alpha * l_ref[j] + jnp.sum(p, axis=-1, keepdims=True)
    acc_ref[j] = acc_ref[j] * _lane_tile(alpha, acc_ref.shape[-1]) + jnp.dot(
        p.astype(BF16), v, preferred_element_type=F32)
    m_ref[j] = m_new


def _attn_prompt_kernel(scal_ref, q_ref, k_ref, v_ref, base_ref, subg_ref, o_ref,
                        m_ref, l_ref, acc_ref, near_ref, *, out_scale):
    h = pl.program_id(0)
    i = pl.program_id(2)
    tq, tk = ATTN_Q_BLOCK, ATTN_K_BLOCK

    @pl.when((pl.program_id(1) == 0) & (i == 0))
    def _():
        for j in range(2):
            rows = jnp.broadcast_to(base_ref[j], (tq, ATTN_BASE_W))
            near_ref[j] = pltpu.roll(rows, 0, 1, stride=1, stride_axis=0)[:, :ATTN_NEAR * tk]

    m_ref[...] = jnp.full_like(m_ref, NEG_INF)
    l_ref[...] = jnp.zeros_like(l_ref)
    acc_ref[...] = jnp.zeros_like(acc_ref)
    q = q_ref[...]
    rel = lax.broadcasted_iota(jnp.int32, (tq, tk), 0) - lax.broadcasted_iota(jnp.int32, (tq, tk), 1)

    def block(kb, bias, min_rel):
        r0 = pl.multiple_of(kb * tk, tk)
        kblk = k_ref[pl.ds(r0, tk), :]
        vblk = v_ref[pl.ds(r0, tk), :]
        for j in range(2):
            s = lax.dot_general(q[:, j * DIFF_HD:(j + 1) * DIFF_HD], kblk[:, j * DIFF_HD:(j + 1) * DIFF_HD],
                                (((1,), (1,)), ((), ())), preferred_element_type=F32) + bias(j)
            if min_rel is not None:
                s = jnp.where(rel >= min_rel, s, NEG_INF)
            _softmax_step(j, s, vblk, m_ref, l_ref, acc_ref)

    first_near = (tq // tk) * i - 1

    def far(kb, carry):
        block(kb, lambda j: scal_ref[1 + 2 * h + j], None)
        return carry

    lax.fori_loop(0, first_near, far, 0)

    @pl.when(i >= 1)
    def _():
        block(first_near, lambda j: near_ref[j, :, 0:tk], None)

    for a in range(1, ATTN_NEAR):
        block(first_near + a, lambda j, a=a: near_ref[j, :, a * tk:(a + 1) * tk], (a - 1) * tk)

    lam = scal_ref[0]
    o = (acc_ref[0] / _lane_tile(l_ref[0], DIFF_VD)) - lam * (acc_ref[1] / _lane_tile(l_ref[1], DIFF_VD))
    o_ref[...] = (_rms(o) * subg_ref[...] * out_scale).astype(o_ref.dtype)


def _attn_prompt(scal, q, k, v, bias_base, sub_g, batch, seq, out_scale):
    m = batch * seq
    tq, tk = ATTN_Q_BLOCK, ATTN_K_BLOCK
    nq = seq // tq
    w = 2 * DIFF_HD
    return pl.pallas_call(
        functools.partial(_attn_prompt_kernel, out_scale=out_scale),
        out_shape=jax.ShapeDtypeStruct((m, DIFF_HEADS * DIFF_VD), BF16),
        grid=(DIFF_HEADS, batch, nq),
        in_specs=[pl.BlockSpec(memory_space=pltpu.SMEM),
                  pl.BlockSpec((tq, w), lambda h, b, i: (b * nq + i, h)),
                  pl.BlockSpec((seq, w), lambda h, b, i: (b, h)),
                  pl.BlockSpec((seq, DIFF_VD), lambda h, b, i: (b, h)),
                  pl.BlockSpec((2, 1, ATTN_BASE_W), lambda h, b, i: (h, 0, 0)),
                  pl.BlockSpec((1, DIFF_VD), lambda h, b, i: (0, 0))],
        out_specs=pl.BlockSpec((tq, DIFF_VD), lambda h, b, i: (b * nq + i, h)),
        scratch_shapes=[pltpu.VMEM((2, tq, LANES), F32), pltpu.VMEM((2, tq, LANES), F32),
                        pltpu.VMEM((2, tq, DIFF_VD), F32), pltpu.VMEM((2, tq, ATTN_NEAR * tk), F32)],
        compiler_params=_cparams(("arbitrary", "arbitrary", "arbitrary")),
        name="attn_prompt",
    )(scal, q, k, v, bias_base, sub_g.reshape(1, DIFF_VD))


SAMPLE_SLOTS = 8
SAMPLE_ROWS = DIFF_HEADS * 2 * SAMPLE_SLOTS


def _attn_sample_kernel(pt_ref, lam_ref, q_ref, kc_ref, v0_ref, v1_ref, kn_ref, vn_ref, bfar_ref, blast_ref,
                        bnew_ref, subg_ref, o_ref, m_ref, l_ref, acc_ref, *, out_scale):
    p = pl.program_id(1)
    n_pages = pl.num_programs(1)
    group = 2 * SAMPLE_SLOTS

    @pl.when(p == 0)
    def _():
        m_ref[...] = jnp.full_like(m_ref, NEG_INF)
        l_ref[...] = jnp.zeros_like(l_ref)
        acc_ref[...] = jnp.zeros_like(acc_ref)

    def attend(k_of_map, v_of_head, bias):
        parts = []
        for hd in range(DIFF_HEADS):
            s0 = lax.dot_general(q_ref[0, 2 * hd], k_of_map(2 * hd), (((1,), (1,)), ((), ())),
                                 preferred_element_type=F32)
            s1 = lax.dot_general(q_ref[0, 2 * hd + 1], k_of_map(2 * hd + 1), (((1,), (1,)), ((), ())),
                                 preferred_element_type=F32)
            parts.append(s0 + s1)
        s = jnp.concatenate(parts, axis=0) + bias
        m_prev = m_ref[...]
        m_new = jnp.maximum(m_prev, jnp.max(s, axis=-1, keepdims=True))
        alpha = jnp.exp(m_prev - m_new)
        pr = jnp.exp(s - m_new)
        l_ref[...] = alpha * l_ref[...] + jnp.sum(pr, axis=-1, keepdims=True)
        pb = pr.astype(BF16)
        pv = jnp.concatenate(
            [jnp.dot(pb[hd * group:(hd + 1) * group], v_of_head(hd), preferred_element_type=F32)
             for hd in range(DIFF_HEADS)], axis=0)
        acc_ref[...] = acc_ref[...] * _lane_tile(alpha, DIFF_VD) + pv
        m_ref[...] = m_new

    def page_k(mp):
        return kc_ref[0, pl.ds(mp, PAGE_SIZE, stride=N_QK_HEADS), :].astype(BF16)

    def page_v(hd):
        rows = pl.ds(hd, PAGE_SIZE, stride=DIFF_HEADS)
        return jnp.concatenate([v0_ref[0, rows, :], v1_ref[0, rows, :]], axis=1).astype(BF16)

    @pl.when(p < n_pages - 1)
    def _():
        attend(page_k, page_v, bfar_ref[...])

    @pl.when(p == n_pages - 1)
    def _():
        attend(page_k, page_v, blast_ref[...])
        attend(lambda mp: kn_ref[0, mp], lambda hd: vn_ref[0, hd], bnew_ref[...])
        a = (acc_ref[...] / _lane_tile(l_ref[...], DIFF_VD)).reshape(DIFF_HEADS, 2, SAMPLE_SLOTS, DIFF_VD)
        d = a[:, 0] - lam_ref[0] * a[:, 1]
        o_ref[0] = _rms(d) * subg_ref[...] * out_scale


def _attn_sample(page_table, lam, q, cache_k, cache_v, k_new, v_new, b_far, b_last, b_new, sub_g, out_scale):
    batch, n_pages = page_table.shape
    const2 = lambda b, p, pt: (0, 0)
    per_b = lambda b, p, pt: (b, 0, 0, 0)
    grid_spec = pltpu.PrefetchScalarGridSpec(
        num_scalar_prefetch=1,
        grid=(batch, n_pages),
        in_specs=[pl.BlockSpec(memory_space=pltpu.SMEM),
                  pl.BlockSpec((1,) + q.shape[1:], per_b),
                  pl.BlockSpec((1, PAGE_SIZE * N_QK_HEADS, DIFF_HD), lambda b, p, pt: (pt[b, p], 0, 0)),
                  pl.BlockSpec((1, PAGE_SIZE * DIFF_HEADS, LANES), lambda b, p, pt: (pt[b, p], 0, 0)),
                  pl.BlockSpec((1, PAGE_SIZE * DIFF_HEADS, LANES), lambda b, p, pt: (pt[b, p], 0, 1)),
                  pl.BlockSpec((1,) + k_new.shape[1:], per_b),
                  pl.BlockSpec((1,) + v_new.shape[1:], per_b),
                  pl.BlockSpec((SAMPLE_ROWS, PAGE_SIZE), const2),
                  pl.BlockSpec((SAMPLE_ROWS, PAGE_SIZE), const2),
                  pl.BlockSpec((SAMPLE_ROWS, PAGE_SIZE), const2),
                  pl.BlockSpec((1, DIFF_VD), const2)],
        out_specs=pl.BlockSpec((1, DIFF_HEADS, SAMPLE_SLOTS, DIFF_VD), per_b),
        scratch_shapes=[pltpu.VMEM((SAMPLE_ROWS, LANES), F32), pltpu.VMEM((SAMPLE_ROWS, LANES), F32),
                        pltpu.VMEM((SAMPLE_ROWS, DIFF_VD), F32)],
    )
    return pl.pallas_call(
        functools.partial(_attn_sample_kernel, out_scale=out_scale),
        out_shape=jax.ShapeDtypeStruct((batch, DIFF_HEADS, SAMPLE_SLOTS, DIFF_VD), F32),
        grid_spec=grid_spec,
        compiler_params=_cparams(("parallel", "arbitrary")),
        name="attn_sample",
    )(page_table, lam, q, cache_k, cache_v, cache_v, k_new, v_new, b_far, b_last, b_new,
      sub_g.reshape(1, DIFF_VD))


def _rope_tables(pos):
    half = RET_DK // 2
    inv = ROPE_BASE ** (-jnp.arange(half, dtype=F32) / half)
    ang = pos.astype(F32)[:, None] * inv[None, :]
    return jnp.cos(ang), jnp.sin(ang)


def _log_decay():
    return jnp.log1p(-jnp.exp2(-5.0 - jnp.arange(RET_HEADS, dtype=F32)))


def _bias_by_distance(n, table):
    max_exact = N_BUCKETS // 2
    nf = jnp.maximum(n, max_exact).astype(F32)
    large = max_exact + (jnp.log(nf / max_exact) / math.log(MAX_DISTANCE / max_exact)
                         * (N_BUCKETS - max_exact)).astype(jnp.int32)
    bucket = jnp.where(n < max_exact, n, jnp.minimum(large, N_BUCKETS - 1))
    return table[bucket].astype(F32)


def _lambda_init(layer_idx):
    return 0.8 - 0.6 * math.exp(-0.3 * layer_idx)


def _lam(lam_p):
    lp = lam_p.astype(F32)
    return jnp.exp(jnp.sum(lp[0] * lp[1])) - jnp.exp(jnp.sum(lp[2] * lp[3]))


def _trunk(x, batch, seq, tm, rope, mixers, wts):
    (norm_gains, w_ret_in, w_ret_out, w_kv, kv_norm, w_q, w_o, w_up, w_down) = wts
    tn = 1024
    h = _norm_cast(x, norm_gains[0, 0], tm)
    ret_states = []
    k32 = v32 = None
    for l in range(DEPTH):
        g = norm_gains[l]
        if l < N_RET_LAYERS:
            w_in = w_ret_in[l]
            (qk,) = _mm_cols(h, w_in, 0, 2 * RET_QK_DIM, [BF16], tm, tn, epilogue="rope",
                             scale=RET_DK ** -0.5, rope=rope, k_tile_start=RET_QK_DIM // tn)
            (v,) = _mm_cols(h, w_in, 2 * RET_QK_DIM, RET_V_DIM, [BF16], tm, tn)
            (gate,) = _mm_cols(h, w_in, 2 * RET_QK_DIM + RET_V_DIM, RET_V_DIM, [F32], tm, tn)
            o, s = mixers["ret"](l, qk, v, gate)
            ret_states.append(s)
            x, (h,) = _mm_resid(o, w_ret_out[l], x, g[1], [g[2]], min(tm, 512), 1024)
        else:
            i = l - N_RET_LAYERS
            (q,) = _mm_cols(h, w_q[i], 0, N_QK_HEADS * DIFF_HD, [BF16], tm, tn, epilogue="scale",
                            scale=DIFF_HD ** -0.5)
            o = mixers["attn"](i, l, q, k16, v16)
            x, (h,) = _mm_resid(o, w_o[i], x, g[1], [g[2]], min(tm, 512), 1024)
        (u,) = _mm_cols(h, w_up[l], 0, D_FF, [BF16], tm, tn, epilogue="relu2")
        if l == N_RET_LAYERS - 1:
            x, (h, hkv) = _mm_resid(u, w_down[l], x, g[3], [norm_gains[l + 1, 0], kv_norm], min(tm, 512), 1024)
            k32, k16 = _mm_cols(hkv, w_kv, 0, N_QK_HEADS * DIFF_HD, [F32, BF16], tm, tn)
            v32, v16 = _mm_cols(hkv, w_kv, N_QK_HEADS * DIFF_HD, DIFF_HEADS * DIFF_VD, [F32, BF16], tm, tn)
            k16, v16 = mixers["kv"](k32, v32, k16, v16)
        elif l < DEPTH - 1:
            x, (h,) = _mm_resid(u, w_down[l], x, g[3], [norm_gains[l + 1, 0]], min(tm, 512), 1024)
        else:
            x, _ = _mm_resid(u, w_down[l], x, g[3], [], min(tm, 512), 1024)
    return x, jnp.stack(ret_states), k32, v32


def kernel(x_prompt, x_sample, state_ret, cache_k, cache_v, page_table, norm_gains, w_ret_in, w_ret_out,
           w_kv, kv_norm, w_q_diff, lam_params, subln_gain, w_o_diff, rel_bias_table, w_up, w_down):
    batch, seq, d = x_prompt.shape
    dec_batch, dec_seq, _ = x_sample.shape
    n_pages = page_table.shape[1]
    past_len = n_pages * PAGE_SIZE

    wts = (norm_gains, w_ret_in, w_ret_out.astype(BF16), w_kv, kv_norm,
           w_q_diff, w_o_diff.astype(BF16), w_up, w_down.astype(BF16))
    log_decay = _log_decay()
    lams = [_lam(lam_params[i]) + _lambda_init(N_RET_LAYERS + i) for i in range(DEPTH - N_RET_LAYERS)]
    out_scales = [1.0 - _lambda_init(N_RET_LAYERS + i) for i in range(DEPTH - N_RET_LAYERS)]

    base_dist = (ATTN_K_BLOCK - jnp.arange(ATTN_BASE_W)) % ATTN_BASE_W
    bias_base = _bias_by_distance(base_dist, rel_bias_table).T.reshape(N_QK_HEADS, 1, ATTN_BASE_W)
    bias_far = _bias_by_distance(jnp.full((), ATTN_K_BLOCK + 1, jnp.int32), rel_bias_table)

    def p_ret(l, qk, v, gate):
        return _ret_prompt(log_decay, qk, v, gate, batch, seq)

    def p_attn(i, l, q, k16, v16):
        scal = jnp.concatenate([lams[i][None], bias_far]).astype(F32)
        return _attn_prompt(scal, q, k16, v16, bias_base, subln_gain[i], batch, seq, out_scales[i])

    y_p, ret_p, k_p, v_p = _trunk(
        x_prompt.reshape(batch * seq, d), batch, seq, math.gcd(seq, 1024), _rope_tables(jnp.arange(seq)),
        {"ret": p_ret, "attn": p_attn, "kv": lambda k32, v32, k16, v16: (k16, v16)}, wts)

    rows_s = dec_batch * dec_seq
    cos_s, sin_s = _rope_tables(past_len + jnp.arange(dec_seq))
    rope_s = (jnp.tile(cos_s, (dec_batch, 1)), jnp.tile(sin_s, (dec_batch, 1)))
    assert dec_seq <= SAMPLE_SLOTS
    ck = cache_k.reshape(cache_k.shape[0], PAGE_SIZE * N_QK_HEADS, DIFF_HD)
    cv = cache_v.reshape(cache_v.shape[0], PAGE_SIZE * DIFF_HEADS, DIFF_VD)
    r = jnp.arange(SAMPLE_ROWS)
    row_t = r % SAMPLE_SLOTS
    row_map = r // SAMPLE_SLOTS
    live = (row_t < dec_seq)[:, None]
    cols = jnp.arange(PAGE_SIZE)
    pick = lambda bias: jnp.take_along_axis(bias, row_map[:, None, None], axis=2)[:, :, 0]
    b_far = jnp.where(live, _bias_by_distance(jnp.full((), past_len, jnp.int32), rel_bias_table)[row_map][:, None],
                      jnp.zeros((1, PAGE_SIZE), F32))
    d_last = (past_len + row_t)[:, None] - (past_len - PAGE_SIZE + cols)[None, :]
    b_last = jnp.where(live, pick(_bias_by_distance(d_last, rel_bias_table)), 0.0)
    d_new = row_t[:, None] - cols[None, :]
    b_new = jnp.where(live, jnp.where(d_new >= 0, pick(_bias_by_distance(jnp.maximum(d_new, 0), rel_bias_table)),
                                      NEG_INF), 0.0)

    def s_ret(l, qk, v, gate):
        return _ret_sample(log_decay, qk, v, gate, state_ret[l], dec_batch, dec_seq)

    new_kv = {}

    def s_kv(k32, v32, k16, v16):
        tok_pad = ((0, 0), (0, 0), (0, PAGE_SIZE - dec_seq), (0, 0))
        new_kv["k"] = jnp.pad(k16.reshape(dec_batch, dec_seq, N_QK_HEADS, DIFF_HD).transpose(0, 2, 1, 3), tok_pad)
        new_kv["v"] = jnp.pad(v16.reshape(dec_batch, dec_seq, DIFF_HEADS, DIFF_VD).transpose(0, 2, 1, 3), tok_pad)
        return k16, v16

    def s_attn(i, l, q, k16, v16):
        qr = q.reshape(dec_batch, dec_seq, DIFF_HEADS, 2, DIFF_HD).transpose(0, 2, 3, 1, 4)
        qr = jnp.pad(qr, ((0, 0), (0, 0), (0, 0), (0, SAMPLE_SLOTS - dec_seq), (0, 0)))
        zero = jnp.zeros_like(qr[:, :, 0])
        q16 = jnp.stack([jnp.concatenate([qr[:, :, 0], zero], axis=2),
                         jnp.concatenate([zero, qr[:, :, 1]], axis=2)], axis=2)
        q16 = q16.reshape(dec_batch, N_QK_HEADS, 2 * SAMPLE_SLOTS, DIFF_HD)
        o = _attn_sample(page_table, lams[i][None].astype(F32), q16, ck, cv, new_kv["k"], new_kv["v"],
                         b_far, b_last, b_new, subln_gain[i], out_scales[i])
        o = o[:, :, :dec_seq].transpose(0, 2, 1, 3)
        return o.reshape(rows_s, DIFF_HEADS * DIFF_VD).astype(BF16)

    y_s, ret_s, k_s, v_s = _trunk(
        x_sample.reshape(rows_s, d), dec_batch, dec_seq, rows_s, rope_s,
        {"ret": s_ret, "attn": s_attn, "kv": s_kv}, wts)

    return (y_p.reshape(batch, seq, d), y_s.reshape(dec_batch, dec_seq, d), ret_p,
            k_p.reshape(batch, seq, N_QK_HEADS, DIFF_HD), v_p.reshape(batch, seq, DIFF_HEADS, DIFF_VD),
            ret_s, k_s.reshape(dec_batch, dec_seq, N_QK_HEADS, DIFF_HD),
            v_s.reshape(dec_batch, dec_seq, DIFF_HEADS, DIFF_VD))
```
